```python
import math
import jax, jax.numpy as jnp
from jax import lax
import numpy as np

D_MODEL = 4096
BATCH = 2
SEQ = 4096
DEPTH = 2
DEC_BATCH = 1
DEC_SEQ = 8192
PAST_LEN = 128

D_MIX = D_MODEL
SSD_INNER = 3 * D_MIX // 8
SSD_HEADDIM = 64
SSD_HEADS = SSD_INNER // SSD_HEADDIM
SSD_GROUPS = 4
SSD_STATE = 128
SSD_CONV = 5
SSD_CHUNK = 128
CONV_CH = SSD_INNER + 2 * SSD_GROUPS * SSD_STATE
DIFF_WIDTH = D_MIX // 4
DIFF_HEAD_DIM = 128
DIFF_HEADS = DIFF_WIDTH // DIFF_HEAD_DIM
DIFF_QK_DIM = DIFF_HEAD_DIM // 2
DIL_WIDTH = D_MIX - SSD_INNER - DIFF_WIDTH
DIL_HEAD_DIM = 128
DIL_HEADS = DIL_WIDTH // DIL_HEAD_DIM
DIL_WINDOWS = (128, 512, 2048)
DIL_RATES = (1, 4, 16)
ATTN_QBLOCK = 128
PROJ_SIZES = (SSD_INNER, CONV_CH, 2 * SSD_HEADS, DIFF_WIDTH, DIFF_WIDTH, DIFF_WIDTH, DIL_WIDTH, DIL_WIDTH, DIL_WIDTH)
PROJ_WIDTH = sum(PROJ_SIZES)
PEER_HEADS = 8
PEER_NKEYS = 128
PEER_EXPERTS = PEER_NKEYS * PEER_NKEYS
PEER_TOPK = 16
PEER_DKEY = 256
PEER_TOKEN_BLOCK = 64
LN_EPS = 1e-5
RMS_EPS = 1e-5
ALPHA = (2 * DEPTH) ** 0.25
BETA = (8 * DEPTH) ** -0.25

kernel_name = 'hymba_ssd_diff_dilated_peer_encoder'


def layer_norm(x, g, b):
    xf = x.astype(jnp.float32)
    mu = jnp.mean(xf, axis=-1, keepdims=True)
    var = jnp.mean(jnp.square(xf - mu), axis=-1, keepdims=True)
    return ((xf - mu) * lax.rsqrt(var + LN_EPS) * g + b).astype(x.dtype)


def rms_norm(x, g):
    xf = x.astype(jnp.float32)
    return xf * lax.rsqrt(jnp.mean(jnp.square(xf), axis=-1, keepdims=True) + RMS_EPS) * g


def alibi_slopes(n):
    return jnp.exp2(-8.0 * jnp.arange(1, n + 1, dtype=jnp.float32) / n)


def centred_dwconv(x, w, b):
    pad = (w.shape[0] - 1) // 2
    y = lax.conv_general_dilated(x, w[:, None, :].astype(x.dtype), window_strides=(1,),
                                 padding=((pad, pad),), dimension_numbers=('NWC', 'WIO', 'NWC'),
                                 feature_group_count=x.shape[-1])
    return y + b


def ssd_chunked(x, dt, a, b, c):
    bsz, t, nh, p = x.shape
    g, n = b.shape[2], b.shape[3]
    r = nh // g
    q = SSD_CHUNK
    nc = t // q
    dtf = dt.astype(jnp.float32)
    xd = (x.astype(jnp.float32) * dtf[..., None]).reshape(bsz, nc, q, g, r, p)
    log_a = (dtf * a.astype(jnp.float32)).reshape(bsz, nc, q, g, r)
    bc = b.astype(jnp.float32).reshape(bsz, nc, q, g, n)
    cc = c.astype(jnp.float32).reshape(bsz, nc, q, g, n)
    acum = jnp.cumsum(log_a, axis=2)
    seg = acum[:, :, :, None] - acum[:, :, None, :]
    lower = jnp.tril(jnp.ones((q, q), dtype=bool))[None, None, :, :, None, None]
    decay = jnp.exp(jnp.where(lower, seg, -jnp.inf))
    cb = jnp.einsum('bclgn,bcsgn->bclsg', cc, bc)
    y_diag = jnp.einsum('bclsgr,bcsgrp->bclgrp', cb[..., None] * decay, xd)
    end_decay = jnp.exp(acum[:, :, -1:] - acum)
    chunk_states = jnp.einsum('bclgn,bclgr,bclgrp->bcgrpn', bc, end_decay, xd)
    chunk_decay = jnp.exp(acum[:, :, -1])

    def step(h, inp):
        st, dec = inp
        return h * dec[..., None, None] + st, h

    h0 = jnp.zeros((bsz, g, r, p, n), jnp.float32)
    _, h_in = lax.scan(step, h0, (jnp.moveaxis(chunk_states, 1, 0), jnp.moveaxis(chunk_decay, 1, 0)))
    h_in = jnp.moveaxis(h_in, 0, 1)
    y_off = jnp.einsum('bclgn,bcgrpn,bclgr->bclgrp', cc, h_in, jnp.exp(acum))
    return (y_diag + y_off).reshape(bsz, t, nh, p)


def ssd_mixer(z, xbc, dt_raw, conv_w, conv_b, dt_bias, a_log, d_skip, norm_g):
    bsz, t, _ = z.shape
    xbc = jax.nn.silu(centred_dwconv(xbc, conv_w, conv_b))
    xs, b, c = jnp.split(xbc, [SSD_INNER, SSD_INNER + SSD_GROUPS * SSD_STATE], axis=-1)
    xs = xs.reshape(bsz, t, SSD_HEADS, SSD_HEADDIM)
    b = b.reshape(bsz, t, SSD_GROUPS, SSD_STATE)
    c = c.reshape(bsz, t, SSD_GROUPS, SSD_STATE)
    dt = jax.nn.softplus(dt_raw.astype(jnp.float32).reshape(bsz, t, 2, SSD_HEADS) + dt_bias.astype(jnp.float32))
    a = -jnp.exp(a_log.astype(jnp.float32))
    flip = lambda u: jnp.flip(u, axis=1)
    y_fwd = ssd_chunked(xs, dt[:, :, 0], a[0], b, c)
    y_bwd = flip(ssd_chunked(flip(xs), flip(dt[:, :, 1]), a[1], flip(b), flip(c)))
    y = y_fwd + y_bwd + d_skip.astype(jnp.float32)[:, None] * xs.astype(jnp.float32)
    y = y.reshape(bsz, t, SSD_INNER) * jax.nn.silu(z.astype(jnp.float32))
    return rms_norm(y, norm_g).astype(z.dtype)


def diff_attention(q, k, v, lambdas, norm_g, lambda_init):
    bsz, t, _ = q.shape
    q = q.reshape(bsz, t, DIFF_HEADS, 2, DIFF_QK_DIM)
    k = k.reshape(bsz, t, DIFF_HEADS, 2, DIFF_QK_DIM)
    v = v.reshape(bsz, t, DIFF_HEADS, DIFF_HEAD_DIM)
    lf = lambdas.astype(jnp.float32)
    lam = jnp.exp(jnp.sum(lf[0] * lf[1])) - jnp.exp(jnp.sum(lf[2] * lf[3])) + lambda_init
    slopes = alibi_slopes(DIFF_HEADS)
    scale = DIFF_QK_DIM ** -0.5
    nb = t // ATTN_QBLOCK
    qb = jnp.moveaxis(q.reshape(bsz, nb, ATTN_QBLOCK, DIFF_HEADS, 2, DIFF_QK_DIM), 1, 0)
    kpos = jnp.arange(t)

    def block(args):
        qblk, start = args
        s = jnp.einsum('bqhmd,bkhmd->bhmqk', qblk, k).astype(jnp.float32) * scale
        qpos = start + jnp.arange(ATTN_QBLOCK)
        dist = jnp.abs(qpos[:, None] - kpos[None, :]).astype(jnp.float32)
        p = jax.nn.softmax(s - (slopes[:, None, None] * dist)[None, :, None], axis=-1)
        w = p[:, :, 0] - lam * p[:, :, 1]
        return jnp.einsum('bhqk,bkhd->bqhd', w.astype(v.dtype), v)

    o = lax.map(block, (qb, jnp.arange(nb) * ATTN_QBLOCK))
    o = jnp.moveaxis(o, 0, 1).reshape(bsz, t, DIFF_HEADS, DIFF_HEAD_DIM)
    o = rms_norm(o, norm_g) * (1.0 - lambda_init)
    return o.reshape(bsz, t, DIFF_WIDTH).astype(q.dtype)


def dilated_branch(q, k, v, rate, radius, slopes):
    bsz, t, nh, dh = q.shape
    length = t // rate
    nb = -(-length // radius)
    lp = nb * radius

    def residues(u):
        return jnp.moveaxis(u.reshape(bsz, length, rate, nh, dh), 2, 1).reshape(bsz * rate, length, nh, dh)

    qr, kr, vr = residues(q), residues(k), residues(v)
    qb = jnp.pad(qr, ((0, 0), (0, lp - length), (0, 0), (0, 0))).reshape(bsz * rate, nb, radius, nh, dh)
    pad_kv = ((0, 0), (radius, lp - length + radius), (0, 0), (0, 0))

    def neighbours(u):
        ub = jnp.pad(u, pad_kv).reshape(bsz * rate, nb + 2, radius, nh, dh)
        return jnp.concatenate([ub[:, :-2], ub[:, 1:-1], ub[:, 2:]], axis=2)

    kb, vb = neighbours(kr), neighbours(vr)
    s = jnp.einsum('gnqhd,gnkhd->gnhqk', qb, kb).astype(jnp.float32) * (DIL_HEAD_DIM ** -0.5)
    blk = jnp.arange(nb)[:, None, None] * radius
    qpos = blk + jnp.arange(radius)[None, :, None]
    kpos = blk - radius + jnp.arange(3 * radius)[None, None, :]
    rel = jnp.abs(kpos - qpos)
    valid = (rel <= radius) & (kpos >= 0) & (kpos < length)
    bias = -slopes[:, None, None, None] * (rate * rel).astype(jnp.float32)
    s = jnp.where(valid[None, :, None], s + jnp.moveaxis(bias, 0, 1)[None], -jnp.inf)
    m = jnp.max(s, axis=-1, keepdims=True)
    e = jnp.exp(s - m)
    den = jnp.sum(e, axis=-1, keepdims=True)
    o = jnp.einsum('gnhqk,gnkhd->gnqhd', (e / den).astype(v.dtype), vb)
    lse = (m + jnp.log(den))[..., 0]
    o = o.reshape(bsz * rate, lp, nh, dh)[:, :length]
    lse = jnp.moveaxis(lse, 2, 3).reshape(bsz * rate, lp, nh)[:, :length]

    def positions(u):
        return jnp.moveaxis(u.reshape(bsz, rate, length, *u.shape[2:]), 1, 2).reshape(bsz, t, *u.shape[2:])

    return positions(o), positions(lse)


def dilated_attention(q, k, v):
    bsz, t, _ = q.shape
    q = q.reshape(bsz, t, DIL_HEADS, DIL_HEAD_DIM)
    k = k.reshape(bsz, t, DIL_HEADS, DIL_HEAD_DIM)
    v = v.reshape(bsz, t, DIL_HEADS, DIL_HEAD_DIM)
    slopes = alibi_slopes(DIL_HEADS)
    outs, lses = [], []
    for window, rate in zip(DIL_WINDOWS, DIL_RATES):
        o, lse = dilated_branch(q, k, v, rate, window // (2 * rate), slopes)
        outs.append(o.astype(jnp.float32))
        lses.append(lse)
    w = jax.nn.softmax(jnp.stack(lses, axis=0), axis=0)
    o = jnp.sum(w[..., None] * jnp.stack(outs, axis=0), axis=0)
    return o.reshape(bsz, t, DIL_WIDTH).astype(q.dtype)


def token_mixer(h, w_in, conv_w, conv_b, dt_bias, a_log, d_skip, ssd_norm_g, diff_lambda, diff_norm_g, w_out, lambda_init):
    proj = h @ w_in
    idx, acc = [], 0
    for size in PROJ_SIZES[:-1]:
        acc += size
        idx.append(acc)
    z, xbc, dt_raw, dq, dk, dv, lq, lk, lv = jnp.split(proj, idx, axis=-1)
    y_ssd = ssd_mixer(z, xbc, dt_raw, conv_w, conv_b, dt_bias, a_log, d_skip, ssd_norm_g)
    y_diff = diff_attention(dq, dk, dv, diff_lambda, diff_norm_g, lambda_init)
    y_dil = dilated_attention(lq, lk, lv)
    cat = jnp.concatenate([y_ssd, y_diff, y_dil], axis=-1).astype(h.dtype)
    return cat @ w_out


def peer(h, wq, keys, u, v):
    bsz, t, d = h.shape
    xt = h.reshape(-1, PEER_TOKEN_BLOCK, d)

    def block(xb):
        q = (xb @ wq).reshape(PEER_TOKEN_BLOCK, PEER_HEADS, 2, PEER_DKEY // 2)
        s = jnp.einsum('thcd,hckd->thck', q, keys).astype(jnp.float32)
        s1, i1 = lax.top_k(s[:, :, 0], PEER_TOPK)
        s2, i2 = lax.top_k(s[:, :, 1], PEER_TOPK)
        cand = (s1[..., :, None] + s2[..., None, :]).reshape(PEER_TOKEN_BLOCK, PEER_HEADS, PEER_TOPK * PEER_TOPK)
        cand_idx = (i1[..., :, None] * PEER_NKEYS + i2[..., None, :]).reshape(PEER_TOKEN_BLOCK, PEER_HEADS, PEER_TOPK * PEER_TOPK)
        top, pos = lax.top_k(cand, PEER_TOPK)
        eidx = jnp.take_along_axis(cand_idx, pos, axis=-1)
        gate = jax.nn.softmax(top, axis=-1)
        u_e = jnp.take(u, eidx, axis=0)
        act = jax.nn.gelu(jnp.einsum('thkd,td->thk', u_e, xb).astype(jnp.float32), approximate=False)
        v_e = jnp.take(v, eidx, axis=0)
        return jnp.einsum('thk,thkd->td', (gate * act).astype(v.dtype), v_e)

    out = lax.map(block, xt)
    return out.reshape(bsz, t, d)


def trunk(x, ln_in_g, ln_in_b, w_in, conv_w, conv_b, dt_bias, a_log, d_skip, ssd_norm_g, diff_lambda,
          diff_norm_g, w_out, ln1_g, ln1_b, peer_wq, peer_keys, peer_u, peer_v, ln2_g, ln2_b):
    h = layer_norm(x, ln_in_g, ln_in_b)
    for l in range(DEPTH):
        lambda_init = 0.8 - 0.6 * math.exp(-0.3 * l)
        mix = token_mixer(h, w_in[l], conv_w[l], conv_b[l], dt_bias[l], a_log[l], d_skip[l], ssd_norm_g[l],
                          diff_lambda[l], diff_norm_g[l], w_out[l], lambda_init)
        h = layer_norm(ALPHA * h + mix, ln1_g[l], ln1_b[l])
        h = layer_norm(ALPHA * h + peer(h, peer_wq[l], peer_keys[l], peer_u[l], peer_v[l]), ln2_g[l], ln2_b[l])
    return h


def setup_inputs(seed: int = 0) -> dict:
    key = jax.random.key(seed)
    ks = jax.random.split(key, 24)
    f32 = jnp.float32

    def nrm(k, shape, scale):
        return jax.random.normal(k, shape, f32) * scale

    def gain(k, shape):
        return 1.0 + 0.01 * jax.random.normal(k, shape, f32)

    dt0 = jnp.exp(jax.random.uniform(ks[7], (DEPTH, 2, SSD_HEADS), f32, math.log(1e-3), math.log(1e-1)))
    return {
        'x_prompt': nrm(ks[0], (BATCH, SEQ, D_MODEL), 1.0),
        'x_sample': nrm(ks[1], (DEC_BATCH, DEC_SEQ, D_MODEL), 1.0),
        'ln_in_g': gain(ks[2], (D_MODEL,)),
        'ln_in_b': nrm(ks[3], (D_MODEL,), 0.01),
        'w_in': nrm(ks[4], (DEPTH, D_MODEL, PROJ_WIDTH), D_MODEL ** -0.5),
        'conv_w': nrm(ks[5], (DEPTH, SSD_CONV, CONV_CH), SSD_CONV ** -0.5),
        'conv_b': nrm(ks[6], (DEPTH, CONV_CH), 0.01),
        'dt_bias': dt0 + jnp.log(-jnp.expm1(-dt0)),
        'a_log': jnp.log(jax.random.uniform(ks[8], (DEPTH, 2, SSD_HEADS), f32, 1.0, 16.0)),
        'd_skip': gain(ks[9], (DEPTH, SSD_HEADS)),
        'ssd_norm_g': gain(ks[10], (DEPTH, SSD_INNER)),
        'diff_lambda': nrm(ks[11], (DEPTH, 4, DIFF_QK_DIM), 0.1),
        'diff_norm_g': gain(ks[12], (DEPTH, DIFF_HEAD_DIM)),
        'w_out': nrm(ks[13], (DEPTH, D_MIX, D_MODEL), BETA * D_MIX ** -0.5),
        'ln1_g': gain(ks[14], (DEPTH, D_MODEL)),
        'ln1_b': nrm(ks[15], (DEPTH, D_MODEL), 0.01),
        'peer_wq': nrm(ks[16], (DEPTH, D_MODEL, PEER_HEADS * PEER_DKEY), D_MODEL ** -0.5),
        'peer_keys': nrm(ks[17], (DEPTH, PEER_HEADS, 2, PEER_NKEYS, PEER_DKEY // 2), (PEER_DKEY // 2) ** -0.5),
        'peer_u': nrm(ks[18], (DEPTH, PEER_EXPERTS, D_MODEL), D_MODEL ** -0.5),
        'peer_v': nrm(ks[19], (DEPTH, PEER_EXPERTS, D_MODEL), BETA * PEER_HEADS ** -0.5),
        'ln2_g': gain(ks[20], (DEPTH, D_MODEL)),
        'ln2_b': nrm(ks[21], (DEPTH, D_MODEL), 0.01),
    }


def reference(x_prompt, x_sample, ln_in_g, ln_in_b, w_in, conv_w, conv_b, dt_bias, a_log, d_skip, ssd_norm_g,
              diff_lambda, diff_norm_g, w_out, ln1_g, ln1_b, peer_wq, peer_keys, peer_u, peer_v, ln2_g, ln2_b):
    y_prompt = trunk(x_prompt, ln_in_g, ln_in_b, w_in, conv_w, conv_b, dt_bias, a_log, d_skip, ssd_norm_g,
                     diff_lambda, diff_norm_g, w_out, ln1_g, ln1_b, peer_wq, peer_keys, peer_u, peer_v, ln2_g, ln2_b)
    y_sample = trunk(x_sample, ln_in_g, ln_in_b, w_in, conv_w, conv_b, dt_bias, a_log, d_skip, ssd_norm_g,
                     diff_lambda, diff_norm_g, w_out, ln1_g, ln1_b, peer_wq, peer_keys, peer_u, peer_v, ln2_g, ln2_b)
    return (y_prompt, y_sample)
```

```python
import functools
import math

import jax
import jax.numpy as jnp
from jax import lax
from jax.experimental import pallas as pl
from jax.experimental.pallas import tpu as pltpu

F32 = jnp.float32
BF16 = jnp.bfloat16

D_MODEL = 4096
DEPTH = 2
SSD_INNER = 1536
SSD_HEADDIM = 64
SSD_HEADS = 24
SSD_GROUPS = 4
SSD_STATE = 128
SSD_CONV = 5
SSD_CHUNK = 128
CONV_CH = SSD_INNER + 2 * SSD_GROUPS * SSD_STATE
DIFF_WIDTH = 1024
DIFF_HEAD_DIM = 128
DIFF_HEADS = 8
DIFF_QK_DIM = 64
DIL_WIDTH = 1536
DIL_HEAD_DIM = 128
DIL_HEADS = 12
DIL_WINDOWS = (128, 512, 2048)
DIL_RATES = (1, 4, 16)
DIL_RADIUS = 64
PEER_HEADS = 8
PEER_NKEYS = 128
PEER_EXPERTS = PEER_NKEYS * PEER_NKEYS
PEER_TOPK = 16
PEER_DKEY = 256
LN_EPS = 1e-5
RMS_EPS = 1e-5
ALPHA = (2 * DEPTH) ** 0.25

LANE = 128
DT_PAD = 2 * LANE
PROJ_A = SSD_INNER + CONV_CH + DT_PAD
PROJ_B = 3 * DIFF_WIDTH + 3 * DIL_WIDTH
VMEM_LIMIT = 56 * 1024 * 1024
NEG_INF = float("-inf")


def _cparams(sem):
    return pltpu.CompilerParams(dimension_semantics=sem, vmem_limit_bytes=VMEM_LIMIT)


def _seg_lookup(row0, segs):
    start = jnp.int32(segs[0][0])
    length = jnp.int32(segs[0][1])
    for s, n in segs[1:]:
        inside = row0 >= s
        start = jnp.where(inside, jnp.int32(s), start)
        length = jnp.where(inside, jnp.int32(n), length)
    return start, length


def _layer_norm(x, g, b):
    mu = jnp.mean(x, axis=-1, keepdims=True)
    xc = x - mu
    var = jnp.mean(xc * xc, axis=-1, keepdims=True)
    return xc * lax.rsqrt(var + LN_EPS) * g + b


def _silu(x):
    return x / (1.0 + jnp.exp(-x))


def _ln_kernel(x_ref, g_ref, b_ref, o_ref, ob_ref):
    y = _layer_norm(x_ref[...], g_ref[...], b_ref[...])
    o_ref[...] = y
    ob_ref[...] = y.astype(BF16)


def _ln_in(x, g, b, tm=256):
    n, d = x.shape
    return pl.pallas_call(
        _ln_kernel,
        grid=(n // tm,),
        in_specs=[pl.BlockSpec((tm, d), lambda i: (i, 0)),
                  pl.BlockSpec((1, d), lambda i: (0, 0)),
                  pl.BlockSpec((1, d), lambda i: (0, 0))],
        out_specs=[pl.BlockSpec((tm, d), lambda i: (i, 0)),
                   pl.BlockSpec((tm, d), lambda i: (i, 0))],
        out_shape=[jax.ShapeDtypeStruct((n, d), F32), jax.ShapeDtypeStruct((n, d), BF16)],
        compiler_params=_cparams(("parallel",)),
        name="ln_in",
    )(x, g.reshape(1, d), b.reshape(1, d))


def _mm_kernel(x_ref, w_ref, o_ref):
    o_ref[...] = jnp.dot(x_ref[...], w_ref[...], preferred_element_type=F32).astype(o_ref.dtype)


def _matmul(x, w, out_dtype, tm, tn, name):
    m, k = x.shape
    n = w.shape[1]
    return pl.pallas_call(
        _mm_kernel,
        grid=(m // tm, n // tn),
        in_specs=[pl.BlockSpec((tm, k), lambda i, j: (i, 0)),
                  pl.BlockSpec((k, tn), lambda i, j: (0, j))],
        out_specs=pl.BlockSpec((tm, tn), lambda i, j: (i, j)),
        out_shape=jax.ShapeDtypeStruct((m, n), out_dtype),
        compiler_params=_cparams(("parallel", "arbitrary")),
        name=name,
    )(x, w)


def _conv_kernel(segs, tq, prev_ref, cur_ref, next_ref, w_ref, b_ref, o_ref):
    row0 = pl.program_id(0) * tq
    start, length = _seg_lookup(row0, segs)
    first = row0 == start
    last = row0 + tq == start + length
    prev = jnp.where(first, 0.0, prev_ref[...])
    nxt = jnp.where(last, 0.0, next_ref[...])
    xx = jnp.concatenate([prev, cur_ref[...], nxt], axis=0)
    pad = (SSD_CONV - 1) // 2
    acc = jnp.broadcast_to(b_ref[...], cur_ref.shape)
    for k in range(SSD_CONV):
        off = 8 - pad + k
        acc = acc + w_ref[k:k + 1, :] * xx[off:off + tq, :]
    o_ref[...] = _silu(acc)


def _ssd_conv(proj_a, conv_w, conv_b, segs, tq=256, tc=512):
    n = proj_a.shape[0]
    c0 = SSD_INNER // tc
    nrow8 = n // 8
    hb = tq // 8
    return pl.pallas_call(
        functools.partial(_conv_kernel, segs, tq),
        grid=(n // tq, CONV_CH // tc),
        in_specs=[pl.BlockSpec((8, tc), lambda i, c: (jnp.maximum(i * hb - 1, 0), c0 + c)),
                  pl.BlockSpec((tq, tc), lambda i, c: (i, c0 + c)),
                  pl.BlockSpec((8, tc), lambda i, c: (jnp.minimum((i + 1) * hb, nrow8 - 1), c0 + c)),
                  pl.BlockSpec((SSD_CONV, tc), lambda i, c: (0, c)),
                  pl.BlockSpec((1, tc), lambda i, c: (0, c))],
        out_specs=pl.BlockSpec((tq, tc), lambda i, c: (i, c)),
        out_shape=jax.ShapeDtypeStruct((n, CONV_CH), F32),
        compiler_params=_cparams(("parallel", "parallel")),
        name="ssd_conv",
    )(proj_a, proj_a, proj_a, conv_w, conv_b.reshape(1, CONV_CH))


def _softplus(x):
    return jnp.maximum(x, 0.0) + jnp.log1p(jnp.exp(-jnp.abs(x)))


def _ssd_kernel(direction, final, segs, nchunks, *refs):
    if final:
        (xs_ref, b_ref, c_ref, dt_ref, dtb_ref, alog_ref, yprev_ref, z_ref, dskip_ref, ng_ref,
         o_ref, st_ref, y_scr) = refs
    else:
        xs_ref, b_ref, c_ref, dt_ref, dtb_ref, alog_ref, o_ref, st_ref = refs
    q = SSD_CHUNK
    i = pl.program_id(0)
    ci = i if direction == 0 else nchunks - 1 - i
    row0 = ci * q
    start, length = _seg_lookup(row0, segs)
    reset = (row0 == start) if direction == 0 else (row0 + q == start + length)

    @pl.when(reset)
    def _():
        st_ref[...] = jnp.zeros(st_ref.shape, F32)

    xs = xs_ref[...]
    xs_b = xs.astype(BF16)
    bmat = b_ref[...]
    cmat = c_ref[...].astype(BF16)
    dtp = _softplus(dt_ref[...] + dtb_ref[...])
    la = dtp * (-jnp.exp(alog_ref[...]))
    rr = lax.broadcasted_iota(jnp.int32, (q, q), 0)
    cc = lax.broadcasted_iota(jnp.int32, (q, q), 1)
    tri = (cc <= rr) if direction == 0 else (cc >= rr)
    acum = jnp.dot(tri.astype(F32), la, precision=lax.Precision.HIGHEST,
                   preferred_element_type=F32)
    end = q - 1 if direction == 0 else 0
    acum_end = acum[end:end + 1, :]
    wmat = dtp * jnp.exp(acum_end - acum)
    cdec = jnp.exp(acum_end)
    acum_t = acum.T
    dt_t = dtp.T
    left = cc < SSD_HEADDIM
    nt = (((1,), (1,)), ((), ()))

    def colb(mat, h):
        return jnp.broadcast_to(mat[:, h:h + 1], (q, q))

    ssq = jnp.zeros((q, 1), F32)
    for hp in range(SSD_HEADS // 2):
        g = (2 * hp) // (SSD_HEADS // SSD_GROUPS)
        ha, hb = 2 * hp, 2 * hp + 1
        gs = slice(g * SSD_STATE, (g + 1) * SSD_STATE)
        ps = slice(hp * LANE, (hp + 1) * LANE)
        c_g = cmat[:, gs]
        cb = lax.dot_general(c_g, bmat[:, gs].astype(BF16), nt, preferred_element_type=F32)
        col_a, col_b = colb(acum, ha), colb(acum, hb)
        m_a = (cb * jnp.exp(jnp.where(tri, col_a - acum_t[ha:ha + 1, :], NEG_INF))
               * dt_t[ha:ha + 1, :]).astype(BF16)
        m_b = (cb * jnp.exp(jnp.where(tri, col_b - acum_t[hb:hb + 1, :], NEG_INF))
               * dt_t[hb:hb + 1, :]).astype(BF16)
        xs_p = xs_b[:, ps]
        y_diag = jnp.where(left, jnp.dot(m_a, xs_p, preferred_element_type=F32),
                           jnp.dot(m_b, xs_p, preferred_element_type=F32))
        st = st_ref[hp]
        y_off = (jnp.dot(c_g, st.astype(BF16), preferred_element_type=F32)
                 * jnp.exp(jnp.where(left, col_a, col_b)))
        y = y_diag + y_off
        w_pair = jnp.where(left, colb(wmat, ha), colb(wmat, hb))
        xw = (xs[:, ps] * w_pair).astype(BF16)
        cd = jnp.where(left[0:1, :], cdec[:, ha:ha + 1], cdec[:, hb:hb + 1])
        b_t = bmat[:, gs].T.astype(BF16)
        st_ref[hp] = st * cd + jnp.dot(b_t, xw, preferred_element_type=F32)
        if final:
            y = y + yprev_ref[:, ps] + dskip_ref[:, ps] * xs[:, ps]
            y = y * _silu(z_ref[:, ps])
            ssq = ssq + jnp.sum(y * y, axis=-1, keepdims=True)
            y_scr[:, ps] = y
        else:
            o_ref[:, ps] = y
    if final:
        scale = lax.rsqrt(ssq * (1.0 / SSD_INNER) + RMS_EPS)
        o_ref[...] = (y_scr[...] * scale * ng_ref[...]).astype(o_ref.dtype)


def _ssd_scan(direction, xbc_act, proj_a, dt_bias, a_log, segs, extra=None):
    n = xbc_act.shape[0]
    q = SSD_CHUNK
    nchunks = n // q
    final = extra is not None
    if direction == 0:
        rmap = lambda i: i
    else:
        rmap = lambda i: nchunks - 1 - i
    dt_col = (SSD_INNER + CONV_CH) // LANE + direction
    in_specs = [pl.BlockSpec((q, SSD_INNER), lambda i: (rmap(i), 0)),
                pl.BlockSpec((q, SSD_GROUPS * SSD_STATE), lambda i: (rmap(i), 3)),
                pl.BlockSpec((q, SSD_GROUPS * SSD_STATE), lambda i: (rmap(i), 4)),
                pl.BlockSpec((q, LANE), lambda i: (rmap(i), dt_col)),
                pl.BlockSpec((None, 1, LANE), lambda i: (direction, 0, 0)),
                pl.BlockSpec((None, 1, LANE), lambda i: (direction, 0, 0))]
    args = [xbc_act, xbc_act, xbc_act, proj_a, dt_bias, a_log]
    scratch = [pltpu.VMEM((SSD_HEADS // 2, SSD_STATE, LANE), F32)]
    if final:
        y_prev, d_skip, norm_g = extra
        in_specs += [pl.BlockSpec((q, SSD_INNER), lambda i: (rmap(i), 0)),
                     pl.BlockSpec((q, SSD_INNER), lambda i: (rmap(i), 0)),
                     pl.BlockSpec((1, SSD_INNER), lambda i: (0, 0)),
                     pl.BlockSpec((1, SSD_INNER), lambda i: (0, 0))]
        args += [y_prev, proj_a, d_skip, norm_g]
        scratch.append(pltpu.VMEM((q, SSD_INNER), F32))
        out_dtype = BF16
    else:
        out_dtype = F32
    return pl.pallas_call(
        functools.partial(_ssd_kernel, direction, final, segs, nchunks),
        grid=(nchunks,),
        in_specs=in_specs,
        out_specs=pl.BlockSpec((q, SSD_INNER), lambda i: (rmap(i), 0)),
        out_shape=jax.ShapeDtypeStruct((n, SSD_INNER), out_dtype),
        scratch_shapes=scratch,
        compiler_params=_cparams(("arbitrary",)),
        name="ssd_bwd" if final else "ssd_fwd",
    )(*args)


def _diff_kernel(segs, kblk, tq, tk, lambda_init, q_ref, k_ref, v_ref, slope_ref, lam_ref, g_ref,
                 o_ref):
    row0 = pl.program_id(1) * tq
    start, length = _seg_lookup(row0, segs)
    kbase = start - (row0 // kblk) * kblk
    q0 = row0 - start
    lane = lax.broadcasted_iota(jnp.int32, (tq, DIFF_HEAD_DIM), 1)
    qv = q_ref[...] * jnp.asarray(DIFF_QK_DIM ** -0.5, BF16)
    zero = jnp.zeros_like(qv)
    q1 = jnp.where(lane < DIFF_QK_DIM, qv, zero)
    q2 = jnp.where(lane >= DIFF_QK_DIM, qv, zero)
    base = (lax.broadcasted_iota(jnp.int32, (tq, tk), 0)
            - lax.broadcasted_iota(jnp.int32, (tq, tk), 1))
    slope = slope_ref[...]
    nt = (((1,), (1,)), ((), ()))

    def body(kj, carry):
        m1, l1, a1, m2, l2, a2 = carry
        k0 = kj * tk
        koff = pl.multiple_of(kbase + k0, tk)
        kc = k_ref[pl.ds(koff, tk), :]
        vc = v_ref[pl.ds(koff, tk), :]
        bias = jnp.abs(base + (q0 - k0)).astype(F32) * slope

        def upd(qm, m, l, a):
            s = lax.dot_general(qm, kc, nt, preferred_element_type=F32) - bias
            m_new = jnp.maximum(m, jnp.max(s, axis=-1, keepdims=True))
            p = jnp.exp(s - m_new)
            r = jnp.exp(m - m_new)
            l_new = r * l + jnp.sum(p, axis=-1, keepdims=True)
            a_new = r * a + jnp.dot(p.astype(BF16), vc, preferred_element_type=F32)
            return m_new, l_new, a_new

        m1, l1, a1 = upd(q1, m1, l1, a1)
        m2, l2, a2 = upd(q2, m2, l2, a2)
        return m1, l1, a1, m2, l2, a2

    init_m = jnp.full((tq, 1), NEG_INF, F32)
    init_l = jnp.zeros((tq, 1), F32)
    init_a = jnp.zeros((tq, DIFF_HEAD_DIM), F32)
    m1, l1, a1, m2, l2, a2 = lax.fori_loop(
        0, length // tk, body, (init_m, init_l, init_a, init_m, init_l, init_a))
    lf = lam_ref[...]
    lam = (jnp.exp(jnp.sum(lf[0:1] * lf[1:2], axis=-1, keepdims=True))
           - jnp.exp(jnp.sum(lf[2:3] * lf[3:4], axis=-1, keepdims=True)) + lambda_init)
    o = a1 / l1 - lam * (a2 / l2)
    o = o * lax.rsqrt(jnp.mean(o * o, axis=-1, keepdims=True) + RMS_EPS) * g_ref[...]
    o_ref[...] = (o * (1.0 - lambda_init)).astype(o_ref.dtype)


def _diff_attention(qkv, lambdas, norm_g, lambda_init, segs, tq=256, tk=512):
    n = qkv.shape[0]
    kblk = max(s[1] for s in segs)
    assert all(s[0] // kblk == (s[0] + s[1] - 1) // kblk for s in segs)
    assert all(s[1] % tk == 0 and s[1] % tq == 0 for s in segs)
    hq = DIFF_WIDTH // DIFF_HEAD_DIM
    slopes = jnp.exp2(-8.0 * jnp.arange(1, DIFF_HEADS + 1, dtype=F32) / DIFF_HEADS)
    slopes = jnp.broadcast_to(slopes[:, None, None], (DIFF_HEADS, 1, tk))
    return pl.pallas_call(
        functools.partial(_diff_kernel, segs, kblk, tq, tk, lambda_init),
        grid=(DIFF_HEADS, n // tq),
        in_specs=[pl.BlockSpec((tq, DIFF_HEAD_DIM), lambda h, i: (i, h)),
                  pl.BlockSpec((kblk, DIFF_HEAD_DIM), lambda h, i: ((i * tq) // kblk, hq + h)),
                  pl.BlockSpec((kblk, DIFF_HEAD_DIM), lambda h, i: ((i * tq) // kblk, 2 * hq + h)),
                  pl.BlockSpec((None, 1, tk), lambda h, i: (h, 0, 0)),
                  pl.BlockSpec((4, DIFF_QK_DIM), lambda h, i: (0, 0)),
                  pl.BlockSpec((1, DIFF_HEAD_DIM), lambda h, i: (0, 0))],
        out_specs=pl.BlockSpec((tq, DIFF_HEAD_DIM), lambda h, i: (i, h)),
        out_shape=jax.ShapeDtypeStruct((n, DIFF_WIDTH), BF16),
        compiler_params=_cparams(("parallel", "parallel")),
        name="diff_attn",
    )(qkv, qkv, qkv, slopes, lambdas, norm_g.reshape(1, DIFF_HEAD_DIM))


def _dil_kernel(segs_r, rate, tq, q_ref, kp_ref, kc_ref, kn_ref, vp_ref, vc_ref, vn_ref,
                o_ref, lse_ref):
    rad = DIL_RADIUS
    row0 = pl.program_id(1) * tq
    start, length = _seg_lookup(row0, segs_r)
    qpos0 = row0 - start
    tw = tq + 2 * rad
    kwin = jnp.concatenate([kp_ref[...], kc_ref[...], kn_ref[...]], axis=0)
    vwin = jnp.concatenate([vp_ref[...], vc_ref[...], vn_ref[...]], axis=0)
    rr = lax.broadcasted_iota(jnp.int32, (tq, tw), 0)
    cc = lax.broadcasted_iota(jnp.int32, (tq, tw), 1)
    delta = jnp.abs(cc - rad - rr)
    kpos = qpos0 - rad + cc
    ok = jnp.where(delta <= rad, jnp.where(kpos >= 0, jnp.where(kpos < length, 1, 0), 0), 0)
    valid = ok > 0
    dist = (delta * rate).astype(F32)
    lane = lax.broadcasted_iota(jnp.int32, (tq, LANE), 1)
    lse_tile = jnp.zeros((tq, LANE), F32)
    nt = (((1,), (1,)), ((), ()))
    scale = DIL_HEAD_DIM ** -0.5
    for h in range(DIL_HEADS):
        hs = slice(h * DIL_HEAD_DIM, (h + 1) * DIL_HEAD_DIM)
        slope = 2.0 ** (-8.0 * (h + 1) / DIL_HEADS)
        s = lax.dot_general(q_ref[:, hs], kwin[:, hs], nt, preferred_element_type=F32) * scale
        s = jnp.where(valid, s - slope * dist, NEG_INF)
        m = jnp.max(s, axis=-1, keepdims=True)
        e = jnp.exp(s - m)
        den = jnp.sum(e, axis=-1, keepdims=True)
        o_ref[:, hs] = jnp.dot((e / den).astype(BF16), vwin[:, hs], preferred_element_type=F32)
        lse_tile = jnp.where(lane == h, m + jnp.log(den), lse_tile)
    lse_ref[...] = lse_tile


def _dil_branch(qkv, rate, segs, tq=128):
    n = qkv.shape[0]
    nr = n // rate
    rad = DIL_RADIUS
    assert all(s[0] % (rate * tq) == 0 and s[1] % (rate * tq) == 0 for s in segs)
    segs_r = tuple((s[0] // rate, s[1] // rate) for s in segs)
    cols = PROJ_B // DIL_WIDTH
    qc = 3 * DIFF_WIDTH // DIL_WIDTH
    view = qkv.reshape(nr, rate * PROJ_B)
    hb = tq // rad
    last = nr // rad - 1
    prev = lambda i: jnp.maximum(i * hb - 1, 0)
    nxt = lambda i: jnp.minimum((i + 1) * hb, last)
    o, lse = pl.pallas_call(
        functools.partial(_dil_kernel, segs_r, rate, tq),
        grid=(rate, nr // tq),
        in_specs=[pl.BlockSpec((tq, DIL_WIDTH), lambda r, i: (i, r * cols + qc)),
                  pl.BlockSpec((rad, DIL_WIDTH), lambda r, i: (prev(i), r * cols + qc + 1)),
                  pl.BlockSpec((tq, DIL_WIDTH), lambda r, i: (i, r * cols + qc + 1)),
                  pl.BlockSpec((rad, DIL_WIDTH), lambda r, i: (nxt(i), r * cols + qc + 1)),
                  pl.BlockSpec((rad, DIL_WIDTH), lambda r, i: (prev(i), r * cols + qc + 2)),
                  pl.BlockSpec((tq, DIL_WIDTH), lambda r, i: (i, r * cols + qc + 2)),
                  pl.BlockSpec((rad, DIL_WIDTH), lambda r, i: (nxt(i), r * cols + qc + 2))],
        out_specs=[pl.BlockSpec((tq, DIL_WIDTH), lambda r, i: (i, r)),
                   pl.BlockSpec((tq, LANE), lambda r, i: (i, r))],
        out_shape=[jax.ShapeDtypeStruct((nr, rate * DIL_WIDTH), F32),
                   jax.ShapeDtypeStruct((nr, rate * LANE), F32)],
        compiler_params=_cparams(("parallel", "parallel")),
        name=f"dil_r{rate}",
    )(view, view, view, view, view, view, view)
    return o.reshape(n, DIL_WIDTH), lse.reshape(n, LANE)


def _dil_combine_kernel(o1_ref, o2_ref, o3_ref, l1_ref, l2_ref, l3_ref, y_ref):
    l1, l2, l3 = l1_ref[...], l2_ref[...], l3_ref[...]
    m = jnp.maximum(jnp.maximum(l1, l2), l3)
    e1, e2, e3 = jnp.exp(l1 - m), jnp.exp(l2 - m), jnp.exp(l3 - m)
    den = e1 + e2 + e3
    w1, w2, w3 = e1 / den, e2 / den, e3 / den
    tm = l1.shape[0]
    for h in range(DIL_HEADS):
        hs = slice(h * DIL_HEAD_DIM, (h + 1) * DIL_HEAD_DIM)
        shp = (tm, DIL_HEAD_DIM)
        y = (jnp.broadcast_to(w1[:, h:h + 1], shp) * o1_ref[:, hs]
             + jnp.broadcast_to(w2[:, h:h + 1], shp) * o2_ref[:, hs]
             + jnp.broadcast_to(w3[:, h:h + 1], shp) * o3_ref[:, hs])
        y_ref[:, hs] = y.astype(y_ref.dtype)


def _dilated_attention(qkv, segs, tm=256):
    n = qkv.shape[0]
    outs = [_dil_branch(qkv, rate, segs) for rate in DIL_RATES]
    ospec = pl.BlockSpec((tm, DIL_WIDTH), lambda i: (i, 0))
    lspec = pl.BlockSpec((tm, LANE), lambda i: (i, 0))
    return pl.pallas_call(
        _dil_combine_kernel,
        grid=(n // tm,),
        in_specs=[ospec, ospec, ospec, lspec, lspec, lspec],
        out_specs=ospec,
        out_shape=jax.ShapeDtypeStruct((n, DIL_WIDTH), BF16),
        compiler_params=_cparams(("parallel",)),
        name="dil_combine",
    )(outs[0][0], outs[1][0], outs[2][0], outs[0][1], outs[1][1], outs[2][1])


def _outproj_kernel(nj, tn, ys_ref, yd_ref, yl_ref, w1_ref, w2_ref, w3_ref, h_ref, g_ref, b_ref,
                    o_ref, ob_ref):
    j = pl.program_id(1)
    mix = (jnp.dot(ys_ref[...], w1_ref[...], preferred_element_type=F32)
           + jnp.dot(yd_ref[...], w2_ref[...], preferred_element_type=F32)
           + jnp.dot(yl_ref[...], w3_ref[...], preferred_element_type=F32))
    val = ALPHA * h_ref[...] + mix
    for jj in range(nj):
        @pl.when(j == jj)
        def _(jj=jj):
            o_ref[:, jj * tn:(jj + 1) * tn] = val

    @pl.when(j == nj - 1)
    def _():
        y = _layer_norm(o_ref[...], g_ref[...], b_ref[...])
        o_ref[...] = y
        ob_ref[...] = y.astype(BF16)


def _outproj_ln(y_ssd, y_diff, y_dil, w1, w2, w3, h, g, b, tm=256, tn=1024):
    n, d = h.shape
    nj = d // tn
    return pl.pallas_call(
        functools.partial(_outproj_kernel, nj, tn),
        grid=(n // tm, nj),
        in_specs=[pl.BlockSpec((tm, SSD_INNER), lambda i, j: (i, 0)),
                  pl.BlockSpec((tm, DIFF_WIDTH), lambda i, j: (i, 0)),
                  pl.BlockSpec((tm, DIL_WIDTH), lambda i, j: (i, 0)),
                  pl.BlockSpec((SSD_INNER, tn), lambda i, j: (0, j)),
                  pl.BlockSpec((DIFF_WIDTH, tn), lambda i, j: (0, j)),
                  pl.BlockSpec((DIL_WIDTH, tn), lambda i, j: (0, j)),
                  pl.BlockSpec((tm, tn), lambda i, j: (i, j)),
                  pl.BlockSpec((1, d), lambda i, j: (0, 0)),
                  pl.BlockSpec((1, d), lambda i, j: (0, 0))],
        out_specs=[pl.BlockSpec((tm, d), lambda i, j: (i, 0)),
                   pl.BlockSpec((tm, d), lambda i, j: (i, 0))],
        out_shape=[jax.ShapeDtypeStruct((n, d), F32), jax.ShapeDtypeStruct((n, d), BF16)],
        compiler_params=_cparams(("parallel", "arbitrary")),
        name="outproj_ln",
    )(y_ssd, y_diff, y_dil, w1, w2, w3, h, g.reshape(1, d), b.reshape(1, d))


def _peer_score_kernel(x_ref, w_ref, keys_ref, o_ref):
    qv = jnp.dot(x_ref[...], w_ref[...], preferred_element_type=F32).astype(BF16)
    nt = (((1,), (1,)), ((), ()))
    half = PEER_DKEY // 2
    for c in range(2):
        o_ref[c] = lax.dot_general(keys_ref[c], qv[:, c * half:(c + 1) * half], nt,
                                   preferred_element_type=F32)


def _peer_scores(hb, wq, keys, tm=1024):
    n, d = hb.shape
    half = PEER_DKEY // 2
    return pl.pallas_call(
        _peer_score_kernel,
        grid=(n // tm, PEER_HEADS),
        in_specs=[pl.BlockSpec((tm, d), lambda i, h: (i, 0)),
                  pl.BlockSpec((d, PEER_DKEY), lambda i, h: (0, h)),
                  pl.BlockSpec((None, 2, PEER_NKEYS, half), lambda i, h: (h, 0, 0, 0))],
        out_specs=pl.BlockSpec((None, 2, PEER_NKEYS, tm), lambda i, h: (h, 0, 0, i)),
        out_shape=jax.ShapeDtypeStruct((PEER_HEADS, 2, PEER_NKEYS, n), F32),
        compiler_params=_cparams(("parallel", "arbitrary")),
        name="peer_scores",
    )(hb, wq, keys)


def _top_values(x, count):
    rows = x.shape[0]
    iota = lax.broadcasted_iota(jnp.int32, x.shape, 0)
    vals = []
    for _ in range(count):
        mx = jnp.max(x, axis=0, keepdims=True)
        vals.append(mx)
        first = jnp.min(jnp.where(x == mx, iota, rows), axis=0, keepdims=True)
        x = jnp.where(iota == first, NEG_INF, x)
    return vals


_STAIRCASE = tuple((p, q) for p in range(PEER_TOPK) for q in range(PEER_TOPK)
                   if (p + 1) * (q + 1) <= PEER_TOPK)


def _peer_thresh_kernel(s_ref, a_ref, thr_ref):
    s1 = s_ref[0]
    s2 = s_ref[1]
    a1 = _top_values(s1, PEER_TOPK)
    a2 = _top_values(s2, PEER_TOPK)
    cands = [a1[p] + a2[q] for p, q in _STAIRCASE]
    pad = (-len(cands)) % 8
    cands += [jnp.full_like(cands[0], NEG_INF)] * pad
    tops = _top_values(jnp.concatenate(cands, axis=0), PEER_TOPK)
    m = tops[0]
    z = jnp.zeros_like(m)
    for t in tops:
        z = z + jnp.exp(t - m)
    shift = m + jnp.log(z)
    a_ref[...] = s1 - shift
    thr_ref[...] = tops[-1] - shift


def _peer_thresholds(scores, tl=256):
    n = scores.shape[-1]
    return pl.pallas_call(
        _peer_thresh_kernel,
        grid=(PEER_HEADS, n // tl),
        in_specs=[pl.BlockSpec((None, 2, PEER_NKEYS, tl), lambda h, i: (h, 0, 0, i))],
        out_specs=[pl.BlockSpec((None, PEER_NKEYS, tl), lambda h, i: (h, 0, i)),
                   pl.BlockSpec((None, 1, tl), lambda h, i: (h, 0, i))],
        out_shape=[jax.ShapeDtypeStruct((PEER_HEADS, PEER_NKEYS, n), F32),
                   jax.ShapeDtypeStruct((PEER_HEADS, 1, n), F32)],
        compiler_params=_cparams(("parallel", "parallel")),
        name="peer_thresh",
    )(scores)


def _gelu(x):
    return 0.5 * x * (1.0 + lax.erf(x * (2.0 ** -0.5)))


def _peer_dense_kernel(ni, x_ref, u_ref, vt_ref, a_ref, s2_ref, thr_ref, o_ref):
    e = pl.program_id(1)

    @pl.when(e == 0)
    def _():
        o_ref[...] = jnp.zeros(o_ref.shape, F32)

    nt = (((1,), (1,)), ((), ()))
    act = lax.dot_general(u_ref[...], x_ref[...], nt, preferred_element_type=F32)
    gact = _gelu(act)
    parts = []
    for ii in range(ni):
        idx = e * ni + ii
        w = jnp.zeros((PEER_NKEYS, act.shape[1]), F32)
        for h in range(PEER_HEADS):
            up = a_ref[h, pl.ds(idx, 1), :] + s2_ref[h]
            w = w + jnp.where(up >= thr_ref[h], jnp.exp(up), 0.0)
        parts.append((w * gact[ii * PEER_NKEYS:(ii + 1) * PEER_NKEYS, :]).astype(BF16))
    gmat = jnp.concatenate(parts, axis=0) if ni > 1 else parts[0]
    o_ref[...] += jnp.dot(vt_ref[...], gmat, preferred_element_type=F32)


def _peer_dense(hb, u, vt, a_t, scores, thr, tm=512, te=256):
    n, d = hb.shape
    ni = te // PEER_NKEYS
    return pl.pallas_call(
        functools.partial(_peer_dense_kernel, ni),
        grid=(n // tm, PEER_EXPERTS // te),
        in_specs=[pl.BlockSpec((tm, d), lambda i, e: (i, 0)),
                  pl.BlockSpec((te, d), lambda i, e: (e, 0)),
                  pl.BlockSpec((d, te), lambda i, e: (0, e)),
                  pl.BlockSpec((PEER_HEADS, PEER_NKEYS, tm), lambda i, e: (0, 0, i)),
                  pl.BlockSpec((PEER_HEADS, None, PEER_NKEYS, tm), lambda i, e: (0, 1, 0, i)),
                  pl.BlockSpec((PEER_HEADS, 1, tm), lambda i, e: (0, 0, i))],
        out_specs=pl.BlockSpec((d, tm), lambda i, e: (0, i)),
        out_shape=jax.ShapeDtypeStruct((d, n), F32),
        compiler_params=_cparams(("parallel", "arbitrary")),
        name="peer_dense",
    )(hb, u, vt, a_t, scores, thr)


def _ln2_kernel(h_ref, pt_ref, g_ref, b_ref, o_ref, ob_ref):
    y = _layer_norm(ALPHA * h_ref[...] + pt_ref[...].T, g_ref[...], b_ref[...])
    o_ref[...] = y
    ob_ref[...] = y.astype(BF16)


def _peer_ln(h, peer_t, g, b, tm=256):
    n, d = h.shape
    return pl.pallas_call(
        _ln2_kernel,
        grid=(n // tm,),
        in_specs=[pl.BlockSpec((tm, d), lambda i: (i, 0)),
                  pl.BlockSpec((d, tm), lambda i: (0, i)),
                  pl.BlockSpec((1, d), lambda i: (0, 0)),
                  pl.BlockSpec((1, d), lambda i: (0, 0))],
        out_specs=[pl.BlockSpec((tm, d), lambda i: (i, 0)),
                   pl.BlockSpec((tm, d), lambda i: (i, 0))],
        out_shape=[jax.ShapeDtypeStruct((n, d), F32), jax.ShapeDtypeStruct((n, d), BF16)],
        compiler_params=_cparams(("parallel",)),
        name="peer_ln",
    )(h, peer_t, g.reshape(1, d), b.reshape(1, d))


def _prep_w_in(w):
    base = SSD_INNER + CONV_CH
    dt0 = w[:, base:base + SSD_HEADS]
    dt1 = w[:, base + SSD_HEADS:base + 2 * SSD_HEADS]
    zpad = jnp.zeros((w.shape[0], LANE - SSD_HEADS), w.dtype)
    w_a = jnp.concatenate([w[:, :base], dt0, zpad, dt1, zpad], axis=1).astype(BF16)
    w_b = w[:, base + 2 * SSD_HEADS:].astype(BF16)
    return w_a, w_b


def _pad_heads(p):
    return jnp.pad(p.astype(F32), ((0, 0), (0, LANE - SSD_HEADS)))[:, None, :]


def _token_mixer(h, hb, segs, w_in, conv_w, conv_b, dt_bias, a_log, d_skip, ssd_norm_g, diff_lambda,
                 diff_norm_g, lambda_init):
    w_a, w_b = _prep_w_in(w_in)
    proj_a = _matmul(hb, w_a, F32, 1024, 256, "proj_ssd")
    qkv = _matmul(hb, w_b, BF16, 1024, 512, "proj_attn")
    xbc_act = _ssd_conv(proj_a, conv_w, conv_b, segs)
    dtb, alog = _pad_heads(dt_bias), _pad_heads(a_log)
    y_fwd = _ssd_scan(0, xbc_act, proj_a, dtb, alog, segs)
    dsk = jnp.repeat(d_skip.astype(F32), SSD_HEADDIM).reshape(1, SSD_INNER)
    y_ssd = _ssd_scan(1, xbc_act, proj_a, dtb, alog, segs,
                      extra=(y_fwd, dsk, ssd_norm_g.reshape(1, SSD_INNER)))
    y_diff = _diff_attention(qkv, diff_lambda, diff_norm_g, lambda_init, segs)
    y_dil = _dilated_attention(qkv, segs)
    return y_ssd, y_diff, y_dil


def _peer(h, hb, wq, keys, u, v, g, b):
    scores = _peer_scores(hb, wq.astype(BF16), keys.astype(BF16))
    a_t, thr = _peer_thresholds(scores)
    peer_t = _peer_dense(hb, u.astype(BF16), v.astype(BF16).T, a_t, scores, thr)
    return _peer_ln(h, peer_t, g, b)


def _forward(x, segs, ln_in_g, ln_in_b, w_in, conv_w, conv_b, dt_bias, a_log, d_skip, ssd_norm_g,
             diff_lambda, diff_norm_g, w_out, ln1_g, ln1_b, peer_wq, peer_keys, peer_u, peer_v,
             ln2_g, ln2_b):
    h, hb = _ln_in(x, ln_in_g, ln_in_b)
    for l in range(DEPTH):
        lambda_init = 0.8 - 0.6 * math.exp(-0.3 * l)
        y_ssd, y_diff, y_dil = _token_mixer(
            h, hb, segs, w_in[l], conv_w[l], conv_b[l], dt_bias[l], a_log[l], d_skip[l],
            ssd_norm_g[l], diff_lambda[l], diff_norm_g[l], lambda_init)
        wo = w_out[l].astype(BF16)
        h, hb = _outproj_ln(y_ssd, y_diff, y_dil, wo[:SSD_INNER], wo[SSD_INNER:SSD_INNER + DIFF_WIDTH],
                            wo[SSD_INNER + DIFF_WIDTH:], h, ln1_g[l], ln1_b[l])
        h, hb = _peer(h, hb, peer_wq[l], peer_keys[l], peer_u[l], peer_v[l], ln2_g[l], ln2_b[l])
    return h


def kernel(x_prompt, x_sample, ln_in_g, ln_in_b, w_in, conv_w, conv_b, dt_bias, a_log, d_skip,
           ssd_norm_g, diff_lambda, diff_norm_g, w_out, ln1_g, ln1_b, peer_wq, peer_keys, peer_u,
           peer_v, ln2_g, ln2_b):
    bp, tp, d = x_prompt.shape
    bs, ts, _ = x_sample.shape
    np_, ns = bp * tp, bs * ts
    x = jnp.concatenate([x_prompt.reshape(np_, d), x_sample.reshape(ns, d)], axis=0)
    segs = tuple((b * tp, tp) for b in range(bp)) + tuple((np_ + b * ts, ts) for b in range(bs))
    y = _forward(x, segs, ln_in_g, ln_in_b, w_in, conv_w, conv_b, dt_bias, a_log, d_skip, ssd_norm_g,
                 diff_lambda, diff_norm_g, w_out, ln1_g, ln1_b, peer_wq, peer_keys, peer_u, peer_v,
                 ln2_g, ln2_b)
    return y[:np_].reshape(bp, tp, d), y[np_:].reshape(bs, ts, d)
```

```python
import functools
import math

import jax
import jax.numpy as jnp
from jax import lax
from jax.experimental import pallas as pl
from jax.experimental.pallas import tpu as pltpu

F32 = jnp.float32
BF16 = jnp.bfloat16

D_MODEL = 4096
DEPTH = 2
SSD_INNER = 1536
SSD_HEADDIM = 64
SSD_HEADS = 24
SSD_GROUPS = 4
SSD_STATE = 128
SSD_CONV = 5
SSD_CHUNK = 128
CONV_CH = SSD_INNER + 2 * SSD_GROUPS * SSD_STATE
DIFF_WIDTH = 1024
DIFF_HEAD_DIM = 128
DIFF_HEADS = 8
DIFF_QK_DIM = 64
DIL_WIDTH = 1536
DIL_HEAD_DIM = 128
DIL_HEADS = 12
DIL_WINDOWS = (128, 512, 2048)
DIL_RATES = (1, 4, 16)
DIL_RADIUS = 64
PEER_HEADS = 8
PEER_NKEYS = 128
PEER_EXPERTS = PEER_NKEYS * PEER_NKEYS
PEER_TOPK = 16
PEER_DKEY = 256
LN_EPS = 1e-5
RMS_EPS = 1e-5
ALPHA = (2 * DEPTH) ** 0.25

LANE = 128
DT_PAD = 2 * LANE
PROJ_A = SSD_INNER + CONV_CH + DT_PAD
PROJ_B = 3 * DIFF_WIDTH + 3 * DIL_WIDTH
VMEM_LIMIT = 56 * 1024 * 1024
NEG_INF = float("-inf")


def _cparams(sem):
    return pltpu.CompilerParams(dimension_semantics=sem, vmem_limit_bytes=VMEM_LIMIT)


def _seg_lookup(row0, segs):
    start = jnp.int32(segs[0][0])
    length = jnp.int32(segs[0][1])
    for s, n in segs[1:]:
        inside = row0 >= s
        start = jnp.where(inside, jnp.int32(s), start)
        length = jnp.where(inside, jnp.int32(n), length)
    return start, length


def _layer_norm(x, g, b):
    mu = jnp.mean(x, axis=-1, keepdims=True)
    xc = x - mu
    var = jnp.mean(xc * xc, axis=-1, keepdims=True)
    return xc * lax.rsqrt(var + LN_EPS) * g + b


def _silu(x):
    return x / (1.0 + jnp.exp(-x))


def _ln_kernel(x_ref, g_ref, b_ref, o_ref, ob_ref):
    y = _layer_norm(x_ref[...], g_ref[...], b_ref[...])
    o_ref[...] = y
    ob_ref[...] = y.astype(BF16)


def _ln_in(x, g, b, tm=256):
    n, d = x.shape
    return pl.pallas_call(
        _ln_kernel,
        grid=(n // tm,),
        in_specs=[pl.BlockSpec((tm, d), lambda i: (i, 0)),
                  pl.BlockSpec((1, d), lambda i: (0, 0)),
                  pl.BlockSpec((1, d), lambda i: (0, 0))],
        out_specs=[pl.BlockSpec((tm, d), lambda i: (i, 0)),
                   pl.BlockSpec((tm, d), lambda i: (i, 0))],
        out_shape=[jax.ShapeDtypeStruct((n, d), F32), jax.ShapeDtypeStruct((n, d), BF16)],
        compiler_params=_cparams(("parallel",)),
        name="ln_in",
    )(x, g.reshape(1, d), b.reshape(1, d))


def _mm_kernel(x_ref, w_ref, o_ref):
    o_ref[...] = jnp.dot(x_ref[...], w_ref[...], preferred_element_type=F32).astype(o_ref.dtype)


def _matmul(x, w, out_dtype, tm, tn, name):
    m, k = x.shape
    n = w.shape[1]
    return pl.pallas_call(
        _mm_kernel,
        grid=(m // tm, n // tn),
        in_specs=[pl.BlockSpec((tm, k), lambda i, j: (i, 0)),
                  pl.BlockSpec((k, tn), lambda i, j: (0, j))],
        out_specs=pl.BlockSpec((tm, tn), lambda i, j: (i, j)),
        out_shape=jax.ShapeDtypeStruct((m, n), out_dtype),
        compiler_params=_cparams(("parallel", "arbitrary")),
        name=name,
    )(x, w)


def _conv_kernel(segs, tq, prev_ref, cur_ref, next_ref, w_ref, b_ref, o_ref):
    row0 = pl.program_id(0) * tq
    start, length = _seg_lookup(row0, segs)
    first = row0 == start
    last = row0 + tq == start + length
    prev = jnp.where(first, 0.0, prev_ref[...])
    nxt = jnp.where(last, 0.0, next_ref[...])
    xx = jnp.concatenate([prev, cur_ref[...], nxt], axis=0)
    pad = (SSD_CONV - 1) // 2
    acc = jnp.broadcast_to(b_ref[...], cur_ref.shape)
    for k in range(SSD_CONV):
        off = 8 - pad + k
        acc = acc + w_ref[k:k + 1, :] * xx[off:off + tq, :]
    o_ref[...] = _silu(acc)


def _ssd_conv(proj_a, conv_w, conv_b, segs, tq=256, tc=512):
    n = proj_a.shape[0]
    c0 = SSD_INNER // tc
    nrow8 = n // 8
    hb = tq // 8
    return pl.pallas_call(
        functools.partial(_conv_kernel, segs, tq),
        grid=(n // tq, CONV_CH // tc),
        in_specs=[pl.BlockSpec((8, tc), lambda i, c: (jnp.maximum(i * hb - 1, 0), c0 + c)),
                  pl.BlockSpec((tq, tc), lambda i, c: (i, c0 + c)),
                  pl.BlockSpec((8, tc), lambda i, c: (jnp.minimum((i + 1) * hb, nrow8 - 1), c0 + c)),
                  pl.BlockSpec((SSD_CONV, tc), lambda i, c: (0, c)),
                  pl.BlockSpec((1, tc), lambda i, c: (0, c))],
        out_specs=pl.BlockSpec((tq, tc), lambda i, c: (i, c)),
        out_shape=jax.ShapeDtypeStruct((n, CONV_CH), F32),
        compiler_params=_cparams(("parallel", "parallel")),
        name="ssd_conv",
    )(proj_a, proj_a, proj_a, conv_w, conv_b.reshape(1, CONV_CH))


def _softplus(x):
    return jnp.maximum(x, 0.0) + jnp.log1p(jnp.exp(-jnp.abs(x)))


def _ssd_kernel(direction, final, segs, nchunks, *refs):
    if final:
        (xs_ref, b_ref, c_ref, dt_ref, dtb_ref, alog_ref, yprev_ref, z_ref, dskip_ref, ng_ref,
         o_ref, st_ref, y_scr) = refs
    else:
        xs_ref, b_ref, c_ref, dt_ref, dtb_ref, alog_ref, o_ref, st_ref = refs
    q = SSD_CHUNK
    i = pl.program_id(0)
    ci = i if direction == 0 else nchunks - 1 - i
    row0 = ci * q
    start, length = _seg_lookup(row0, segs)
    reset = (row0 == start) if direction == 0 else (row0 + q == start + length)

    @pl.when(reset)
    def _():
        st_ref[...] = jnp.zeros(st_ref.shape, F32)

    xs = xs_ref[...]
    xs_b = xs.astype(BF16)
    bmat = b_ref[...]
    cmat = c_ref[...].astype(BF16)
    dtp = _softplus(dt_ref[...] + dtb_ref[...])
    la = dtp * (-jnp.exp(alog_ref[...]))
    rr = lax.broadcasted_iota(jnp.int32, (q, q), 0)
    cc = lax.broadcasted_iota(jnp.int32, (q, q), 1)
    tri = (cc <= rr) if direction == 0 else (cc >= rr)
    acum = jnp.dot(tri.astype(F32), la, precision=lax.Precision.HIGHEST,
                   preferred_element_type=F32)
    end = q - 1 if direction == 0 else 0
    acum_end = acum[end:end + 1, :]
    wmat = dtp * jnp.exp(acum_end - acum)
    cdec = jnp.exp(acum_end)
    acum_t = acum.T
    dt_t = dtp.T
    left = cc < SSD_HEADDIM
    nt = (((1,), (1,)), ((), ()))

    def colb(mat, h):
        return jnp.broadcast_to(mat[:, h:h + 1], (q, q))

    ssq = jnp.zeros((q, 1), F32)
    for hp in range(SSD_HEADS // 2):
        g = (2 * hp) // (SSD_HEADS // SSD_GROUPS)
        ha, hb = 2 * hp, 2 * hp + 1
        gs = slice(g * SSD_STATE, (g + 1) * SSD_STATE)
        ps = slice(hp * LANE, (hp + 1) * LANE)
        c_g = cmat[:, gs]
        cb = lax.dot_general(c_g, bmat[:, gs].astype(BF16), nt, preferred_element_type=F32)
        col_a, col_b = colb(acum, ha), colb(acum, hb)
        m_a = (cb * jnp.exp(jnp.where(tri, col_a - acum_t[ha:ha + 1, :], NEG_INF))
               * dt_t[ha:ha + 1, :]).astype(BF16)
        m_b = (cb * jnp.exp(jnp.where(tri, col_b - acum_t[hb:hb + 1, :], NEG_INF))
               * dt_t[hb:hb + 1, :]).astype(BF16)
        xs_p = xs_b[:, ps]
        y_diag = jnp.where(left, jnp.dot(m_a, xs_p, preferred_element_type=F32),
                           jnp.dot(m_b, xs_p, preferred_element_type=F32))
        st = st_ref[hp]
        y_off = (jnp.dot(c_g, st.astype(BF16), preferred_element_type=F32)
                 * jnp.exp(jnp.where(left, col_a, col_b)))
        y = y_diag + y_off
        w_pair = jnp.where(left, colb(wmat, ha), colb(wmat, hb))
        xw = (xs[:, ps] * w_pair).astype(BF16)
        cd = jnp.where(left[0:1, :], cdec[:, ha:ha + 1], cdec[:, hb:hb + 1])
        b_t = bmat[:, gs].T.astype(BF16)
        st_ref[hp] = st * cd + jnp.dot(b_t, xw, preferred_element_type=F32)
        if final:
            y = y + yprev_ref[:, ps] + dskip_ref[:, ps] * xs[:, ps]
            y = y * _silu(z_ref[:, ps])
            ssq = ssq + jnp.sum(y * y, axis=-1, keepdims=True)
            y_scr[:, ps] = y
        else:
            o_ref[:, ps] = y
    if final:
        scale = lax.rsqrt(ssq * (1.0 / SSD_INNER) + RMS_EPS)
        o_ref[...] = (y_scr[...] * scale * ng_ref[...]).astype(o_ref.dtype)


def _ssd_scan(direction, xbc_act, proj_a, dt_bias, a_log, segs, extra=None):
    n = xbc_act.shape[0]
    q = SSD_CHUNK
    nchunks = n // q
    final = extra is not None
    if direction == 0:
        rmap = lambda i: i
    else:
        rmap = lambda i: nchunks - 1 - i
    dt_col = (SSD_INNER + CONV_CH) // LANE + direction
    in_specs = [pl.BlockSpec((q, SSD_INNER), lambda i: (rmap(i), 0)),
                pl.BlockSpec((q, SSD_GROUPS * SSD_STATE), lambda i: (rmap(i), 3)),
                pl.BlockSpec((q, SSD_GROUPS * SSD_STATE), lambda i: (rmap(i), 4)),
                pl.BlockSpec((q, LANE), lambda i: (rmap(i), dt_col)),
                pl.BlockSpec((None, 1, LANE), lambda i: (direction, 0, 0)),
                pl.BlockSpec((None, 1, LANE), lambda i: (direction, 0, 0))]
    args = [xbc_act, xbc_act, xbc_act, proj_a, dt_bias, a_log]
    scratch = [pltpu.VMEM((SSD_HEADS // 2, SSD_STATE, LANE), F32)]
    if final:
        y_prev, d_skip, norm_g = extra
        in_specs += [pl.BlockSpec((q, SSD_INNER), lambda i: (rmap(i), 0)),
                     pl.BlockSpec((q, SSD_INNER), lambda i: (rmap(i), 0)),
                     pl.BlockSpec((1, SSD_INNER), lambda i: (0, 0)),
                     pl.BlockSpec((1, SSD_INNER), lambda i: (0, 0))]
        args += [y_prev, proj_a, d_skip, norm_g]
        scratch.append(pltpu.VMEM((q, SSD_INNER), F32))
        out_dtype = BF16
    else:
        out_dtype = F32
    return pl.pallas_call(
        functools.partial(_ssd_kernel, direction, final, segs, nchunks),
        grid=(nchunks,),
        in_specs=in_specs,
        out_specs=pl.BlockSpec((q, SSD_INNER), lambda i: (rmap(i), 0)),
        out_shape=jax.ShapeDtypeStruct((n, SSD_INNER), out_dtype),
        scratch_shapes=scratch,
        compiler_params=_cparams(("arbitrary",)),
        name="ssd_bwd" if final else "ssd_fwd",
    )(*args)


def _diff_kernel(segs, kblk, tq, tk, lambda_init, q_ref, k_ref, v_ref, slope_ref, lam_ref, g_ref,
                 o_ref):
    row0 = pl.program_id(1) * tq
    start, length = _seg_lookup(row0, segs)
    kbase = start - (row0 // kblk) * kblk
    q0 = row0 - start
    lane = lax.broadcasted_iota(jnp.int32, (tq, DIFF_HEAD_DIM), 1)
    qv = q_ref[...] * jnp.asarray(DIFF_QK_DIM ** -0.5, BF16)
    zero = jnp.zeros_like(qv)
    q1 = jnp.where(lane < DIFF_QK_DIM, qv, zero)
    q2 = jnp.where(lane >= DIFF_QK_DIM, qv, zero)
    base = (lax.broadcasted_iota(jnp.int32, (tq, tk), 0)
            - lax.broadcasted_iota(jnp.int32, (tq, tk), 1))
    slope = slope_ref[...]
    nt = (((1,), (1,)), ((), ()))

    def body(kj, carry):
        m1, l1, a1, m2, l2, a2 = carry
        k0 = kj * tk
        koff = pl.multiple_of(kbase + k0, tk)
        kc = k_ref[pl.ds(koff, tk), :]
        vc = v_ref[pl.ds(koff, tk), :]
        bias = jnp.abs(base + (q0 - k0)).astype(F32) * slope

        def upd(qm, m, l, a):
            s = lax.dot_general(qm, kc, nt, preferred_element_type=F32) - bias
            m_new = jnp.maximum(m, jnp.max(s, axis=-1, keepdims=True))
            p = jnp.exp(s - m_new)
            r = jnp.exp(m - m_new)
            l_new = r * l + jnp.sum(p, axis=-1, keepdims=True)
            a_new = r * a + jnp.dot(p.astype(BF16), vc, preferred_element_type=F32)
            return m_new, l_new, a_new

        m1, l1, a1 = upd(q1, m1, l1, a1)
        m2, l2, a2 = upd(q2, m2, l2, a2)
        return m1, l1, a1, m2, l2, a2

    init_m = jnp.full((tq, 1), NEG_INF, F32)
    init_l = jnp.zeros((tq, 1), F32)
    init_a = jnp.zeros((tq, DIFF_HEAD_DIM), F32)
    m1, l1, a1, m2, l2, a2 = lax.fori_loop(
        0, length // tk, body, (init_m, init_l, init_a, init_m, init_l, init_a))
    lf = lam_ref[...]
    lam = (jnp.exp(jnp.sum(lf[0:1] * lf[1:2], axis=-1, keepdims=True))
           - jnp.exp(jnp.sum(lf[2:3] * lf[3:4], axis=-1, keepdims=True)) + lambda_init)
    o = a1 / l1 - lam * (a2 / l2)
    o = o * lax.rsqrt(jnp.mean(o * o, axis=-1, keepdims=True) + RMS_EPS) * g_ref[...]
    o_ref[...] = (o * (1.0 - lambda_init)).astype(o_ref.dtype)


def _diff_attention(qkv, lambdas, norm_g, lambda_init, segs, tq=256, tk=2048):
    n = qkv.shape[0]
    kblk = max(s[1] for s in segs)
    assert all(s[0] // kblk == (s[0] + s[1] - 1) // kblk for s in segs)
    assert all(s[1] % tk == 0 and s[1] % tq == 0 for s in segs)
    hq = DIFF_WIDTH // DIFF_HEAD_DIM
    slopes = jnp.exp2(-8.0 * jnp.arange(1, DIFF_HEADS + 1, dtype=F32) / DIFF_HEADS)
    slopes = jnp.broadcast_to(slopes[:, None, None], (DIFF_HEADS, 1, tk))
    return pl.pallas_call(
        functools.partial(_diff_kernel, segs, kblk, tq, tk, lambda_init),
        grid=(DIFF_HEADS, n // tq),
        in_specs=[pl.BlockSpec((tq, DIFF_HEAD_DIM), lambda h, i: (i, h)),
                  pl.BlockSpec((kblk, DIFF_HEAD_DIM), lambda h, i: ((i * tq) // kblk, hq + h)),
                  pl.BlockSpec((kblk, DIFF_HEAD_DIM), lambda h, i: ((i * tq) // kblk, 2 * hq + h)),
                  pl.BlockSpec((None, 1, tk), lambda h, i: (h, 0, 0)),
                  pl.BlockSpec((4, DIFF_QK_DIM), lambda h, i: (0, 0)),
                  pl.BlockSpec((1, DIFF_HEAD_DIM), lambda h, i: (0, 0))],
        out_specs=pl.BlockSpec((tq, DIFF_HEAD_DIM), lambda h, i: (i, h)),
        out_shape=jax.ShapeDtypeStruct((n, DIFF_WIDTH), BF16),
        compiler_params=_cparams(("parallel", "parallel")),
        name="diff_attn",
    )(qkv, qkv, qkv, slopes, lambdas, norm_g.reshape(1, DIFF_HEAD_DIM))


def _dil_kernel(segs_r, rate, tq, q_ref, kp_ref, kc_ref, kn_ref, vp_ref, vc_ref, vn_ref,
                o_ref, lse_ref):
    rad = DIL_RADIUS
    row0 = pl.program_id(1) * tq
    start, length = _seg_lookup(row0, segs_r)
    qpos0 = row0 - start
    tw = tq + 2 * rad
    kwin = jnp.concatenate([kp_ref[...], kc_ref[...], kn_ref[...]], axis=0)
    vwin = jnp.concatenate([vp_ref[...], vc_ref[...], vn_ref[...]], axis=0)
    rr = lax.broadcasted_iota(jnp.int32, (tq, tw), 0)
    cc = lax.broadcasted_iota(jnp.int32, (tq, tw), 1)
    delta = jnp.abs(cc - rad - rr)
    kpos = qpos0 - rad + cc
    ok = jnp.where(delta <= rad, jnp.where(kpos >= 0, jnp.where(kpos < length, 1, 0), 0), 0)
    valid = ok > 0
    dist = (delta * rate).astype(F32)
    lane = lax.broadcasted_iota(jnp.int32, (tq, LANE), 1)
    lse_tile = jnp.zeros((tq, LANE), F32)
    nt = (((1,), (1,)), ((), ()))
    scale = DIL_HEAD_DIM ** -0.5
    for h in range(DIL_HEADS):
        hs = slice(h * DIL_HEAD_DIM, (h + 1) * DIL_HEAD_DIM)
        slope = 2.0 ** (-8.0 * (h + 1) / DIL_HEADS)
        s = lax.dot_general(q_ref[:, hs], kwin[:, hs], nt, preferred_element_type=F32) * scale
        s = jnp.where(valid, s - slope * dist, NEG_INF)
        m = jnp.max(s, axis=-1, keepdims=True)
        e = jnp.exp(s - m)
        den = jnp.sum(e, axis=-1, keepdims=True)
        o_ref[:, hs] = jnp.dot((e / den).astype(BF16), vwin[:, hs], preferred_element_type=F32)
        lse_tile = jnp.where(lane == h, m + jnp.log(den), lse_tile)
    lse_ref[...] = lse_tile


def _dil_branch(qkv, rate, segs, tq=128):
    n = qkv.shape[0]
    nr = n // rate
    rad = DIL_RADIUS
    assert all(s[0] % (rate * tq) == 0 and s[1] % (rate * tq) == 0 for s in segs)
    segs_r = tuple((s[0] // rate, s[1] // rate) for s in segs)
    cols = PROJ_B // DIL_WIDTH
    qc = 3 * DIFF_WIDTH // DIL_WIDTH
    view = qkv.reshape(nr, rate * PROJ_B)
    hb = tq // rad
    last = nr // rad - 1
    prev = lambda i: jnp.maximum(i * hb - 1, 0)
    nxt = lambda i: jnp.minimum((i + 1) * hb, last)
    o, lse = pl.pallas_call(
        functools.partial(_dil_kernel, segs_r, rate, tq),
        grid=(rate, nr // tq),
        in_specs=[pl.BlockSpec((tq, DIL_WIDTH), lambda r, i: (i, r * cols + qc)),
                  pl.BlockSpec((rad, DIL_WIDTH), lambda r, i: (prev(i), r * cols + qc + 1)),
                  pl.BlockSpec((tq, DIL_WIDTH), lambda r, i: (i, r * cols + qc + 1)),
                  pl.BlockSpec((rad, DIL_WIDTH), lambda r, i: (nxt(i), r * cols + qc + 1)),
                  pl.BlockSpec((rad, DIL_WIDTH), lambda r, i: (prev(i), r * cols + qc + 2)),
                  pl.BlockSpec((tq, DIL_WIDTH), lambda r, i: (i, r * cols + qc + 2)),
                  pl.BlockSpec((rad, DIL_WIDTH), lambda r, i: (nxt(i), r * cols + qc + 2))],
        out_specs=[pl.BlockSpec((tq, DIL_WIDTH), lambda r, i: (i, r)),
                   pl.BlockSpec((tq, LANE), lambda r, i: (i, r))],
        out_shape=[jax.ShapeDtypeStruct((nr, rate * DIL_WIDTH), F32),
                   jax.ShapeDtypeStruct((nr, rate * LANE), F32)],
        compiler_params=_cparams(("parallel", "parallel")),
        name=f"dil_r{rate}",
    )(view, view, view, view, view, view, view)
    return o.reshape(n, DIL_WIDTH), lse.reshape(n, LANE)


def _dil_combine_kernel(o1_ref, o2_ref, o3_ref, l1_ref, l2_ref, l3_ref, y_ref):
    l1, l2, l3 = l1_ref[...], l2_ref[...], l3_ref[...]
    m = jnp.maximum(jnp.maximum(l1, l2), l3)
    e1, e2, e3 = jnp.exp(l1 - m), jnp.exp(l2 - m), jnp.exp(l3 - m)
    den = e1 + e2 + e3
    w1, w2, w3 = e1 / den, e2 / den, e3 / den
    tm = l1.shape[0]
    for h in range(DIL_HEADS):
        hs = slice(h * DIL_HEAD_DIM, (h + 1) * DIL_HEAD_DIM)
        shp = (tm, DIL_HEAD_DIM)
        y = (jnp.broadcast_to(w1[:, h:h + 1], shp) * o1_ref[:, hs]
             + jnp.broadcast_to(w2[:, h:h + 1], shp) * o2_ref[:, hs]
             + jnp.broadcast_to(w3[:, h:h + 1], shp) * o3_ref[:, hs])
        y_ref[:, hs] = y.astype(y_ref.dtype)


def _dilated_attention(qkv, segs, tm=256):
    n = qkv.shape[0]
    outs = [_dil_branch(qkv, rate, segs) for rate in DIL_RATES]
    ospec = pl.BlockSpec((tm, DIL_WIDTH), lambda i: (i, 0))
    lspec = pl.BlockSpec((tm, LANE), lambda i: (i, 0))
    return pl.pallas_call(
        _dil_combine_kernel,
        grid=(n // tm,),
        in_specs=[ospec, ospec, ospec, lspec, lspec, lspec],
        out_specs=ospec,
        out_shape=jax.ShapeDtypeStruct((n, DIL_WIDTH), BF16),
        compiler_params=_cparams(("parallel",)),
        name="dil_combine",
    )(outs[0][0], outs[1][0], outs[2][0], outs[0][1], outs[1][1], outs[2][1])


def _outproj_kernel(nj, tn, ys_ref, yd_ref, yl_ref, w1_ref, w2_ref, w3_ref, h_ref, g_ref, b_ref,
                    o_ref, ob_ref):
    j = pl.program_id(1)
    mix = (jnp.dot(ys_ref[...], w1_ref[...], preferred_element_type=F32)
           + jnp.dot(yd_ref[...], w2_ref[...], preferred_element_type=F32)
           + jnp.dot(yl_ref[...], w3_ref[...], preferred_element_type=F32))
    val = ALPHA * h_ref[...] + mix
    for jj in range(nj):
        @pl.when(j == jj)
        def _(jj=jj):
            o_ref[:, jj * tn:(jj + 1) * tn] = val

    @pl.when(j == nj - 1)
    def _():
        y = _layer_norm(o_ref[...], g_ref[...], b_ref[...])
        o_ref[...] = y
        ob_ref[...] = y.astype(BF16)


def _outproj_ln(y_ssd, y_diff, y_dil, w1, w2, w3, h, g, b, tm=256, tn=1024):
    n, d = h.shape
    nj = d // tn
    return pl.pallas_call(
        functools.partial(_outproj_kernel, nj, tn),
        grid=(n // tm, nj),
        in_specs=[pl.BlockSpec((tm, SSD_INNER), lambda i, j: (i, 0)),
                  pl.BlockSpec((tm, DIFF_WIDTH), lambda i, j: (i, 0)),
                  pl.BlockSpec((tm, DIL_WIDTH), lambda i, j: (i, 0)),
                  pl.BlockSpec((SSD_INNER, tn), lambda i, j: (0, j)),
                  pl.BlockSpec((DIFF_WIDTH, tn), lambda i, j: (0, j)),
                  pl.BlockSpec((DIL_WIDTH, tn), lambda i, j: (0, j)),
                  pl.BlockSpec((tm, tn), lambda i, j: (i, j)),
                  pl.BlockSpec((1, d), lambda i, j: (0, 0)),
                  pl.BlockSpec((1, d), lambda i, j: (0, 0))],
        out_specs=[pl.BlockSpec((tm, d), lambda i, j: (i, 0)),
                   pl.BlockSpec((tm, d), lambda i, j: (i, 0))],
        out_shape=[jax.ShapeDtypeStruct((n, d), F32), jax.ShapeDtypeStruct((n, d), BF16)],
        compiler_params=_cparams(("parallel", "arbitrary")),
        name="outproj_ln",
    )(y_ssd, y_diff, y_dil, w1, w2, w3, h, g.reshape(1, d), b.reshape(1, d))


def _peer_score_kernel(x_ref, w_ref, keys_ref, o_ref):
    qv = jnp.dot(x_ref[...], w_ref[...], preferred_element_type=F32).astype(BF16)
    nt = (((1,), (1,)), ((), ()))
    half = PEER_DKEY // 2
    for c in range(2):
        o_ref[c] = lax.dot_general(keys_ref[c], qv[:, c * half:(c + 1) * half], nt,
                                   preferred_element_type=F32)


def _peer_scores(hb, wq, keys, tm=1024):
    n, d = hb.shape
    half = PEER_DKEY // 2
    return pl.pallas_call(
        _peer_score_kernel,
        grid=(n // tm, PEER_HEADS),
        in_specs=[pl.BlockSpec((tm, d), lambda i, h: (i, 0)),
                  pl.BlockSpec((d, PEER_DKEY), lambda i, h: (0, h)),
                  pl.BlockSpec((None, 2, PEER_NKEYS, half), lambda i, h: (h, 0, 0, 0))],
        out_specs=pl.BlockSpec((None, 2, PEER_NKEYS, tm), lambda i, h: (h, 0, 0, i)),
        out_shape=jax.ShapeDtypeStruct((PEER_HEADS, 2, PEER_NKEYS, n), F32),
        compiler_params=_cparams(("parallel", "arbitrary")),
        name="peer_scores",
    )(hb, wq, keys)


def _top_values(x, count):
    rows = x.shape[0]
    iota = lax.broadcasted_iota(jnp.int32, x.shape, 0)
    vals = []
    for _ in range(count):
        mx = jnp.max(x, axis=0, keepdims=True)
        vals.append(mx)
        first = jnp.min(jnp.where(x == mx, iota, rows), axis=0, keepdims=True)
        x = jnp.where(iota == first, NEG_INF, x)
    return vals


_STAIRCASE = tuple((p, q) for p in range(PEER_TOPK) for q in range(PEER_TOPK)
                   if (p + 1) * (q + 1) <= PEER_TOPK)


def _peer_thresh_kernel(s_ref, a_ref, thr_ref):
    s1 = s_ref[0]
    s2 = s_ref[1]
    a1 = _top_values(s1, PEER_TOPK)
    a2 = _top_values(s2, PEER_TOPK)
    cands = [a1[p] + a2[q] for p, q in _STAIRCASE]
    pad = (-len(cands)) % 8
    cands += [jnp.full_like(cands[0], NEG_INF)] * pad
    tops = _top_values(jnp.concatenate(cands, axis=0), PEER_TOPK)
    m = tops[0]
    z = jnp.zeros_like(m)
    for t in tops:
        z = z + jnp.exp(t - m)
    shift = m + jnp.log(z)
    a_ref[...] = s1 - shift
    thr_ref[...] = tops[-1] - shift


def _peer_thresholds(scores, tl=256):
    n = scores.shape[-1]
    return pl.pallas_call(
        _peer_thresh_kernel,
        grid=(PEER_HEADS, n // tl),
        in_specs=[pl.BlockSpec((None, 2, PEER_NKEYS, tl), lambda h, i: (h, 0, 0, i))],
        out_specs=[pl.BlockSpec((None, PEER_NKEYS, tl), lambda h, i: (h, 0, i)),
                   pl.BlockSpec((None, 1, tl), lambda h, i: (h, 0, i))],
        out_shape=[jax.ShapeDtypeStruct((PEER_HEADS, PEER_NKEYS, n), F32),
                   jax.ShapeDtypeStruct((PEER_HEADS, 1, n), F32)],
        compiler_params=_cparams(("parallel", "parallel")),
        name="peer_thresh",
    )(scores)


def _gelu(x):
    return 0.5 * x * (1.0 + lax.erf(x * (2.0 ** -0.5)))


def _peer_dense_kernel(ni, x_ref, u_ref, vt_ref, a_ref, s2_ref, thr_ref, o_ref):
    e = pl.program_id(1)

    @pl.when(e == 0)
    def _():
        o_ref[...] = jnp.zeros(o_ref.shape, F32)

    nt = (((1,), (1,)), ((), ()))
    act = lax.dot_general(u_ref[...], x_ref[...], nt, preferred_element_type=F32)
    gact = _gelu(act)
    parts = []
    for ii in range(ni):
        idx = e * ni + ii
        w = jnp.zeros((PEER_NKEYS, act.shape[1]), F32)
        for h in range(PEER_HEADS):
            up = a_ref[h, pl.ds(idx, 1), :] + s2_ref[h]
            w = w + jnp.where(up >= thr_ref[h], jnp.exp(up), 0.0)
        parts.append((w * gact[ii * PEER_NKEYS:(ii + 1) * PEER_NKEYS, :]).astype(BF16))
    gmat = jnp.concatenate(parts, axis=0) if ni > 1 else parts[0]
    o_ref[...] += jnp.dot(vt_ref[...], gmat, preferred_element_type=F32)


def _peer_dense(hb, u, vt, a_t, scores, thr, tm=512, te=512):
    n, d = hb.shape
    ni = te // PEER_NKEYS
    return pl.pallas_call(
        functools.partial(_peer_dense_kernel, ni),
        grid=(n // tm, PEER_EXPERTS // te),
        in_specs=[pl.BlockSpec((tm, d), lambda i, e: (i, 0)),
                  pl.BlockSpec((te, d), lambda i, e: (e, 0)),
                  pl.BlockSpec((d, te), lambda i, e: (0, e)),
                  pl.BlockSpec((PEER_HEADS, PEER_NKEYS, tm), lambda i, e: (0, 0, i)),
                  pl.BlockSpec((PEER_HEADS, None, PEER_NKEYS, tm), lambda i, e: (0, 1, 0, i)),
                  pl.BlockSpec((PEER_HEADS, 1, tm), lambda i, e: (0, 0, i))],
        out_specs=pl.BlockSpec((d, tm), lambda i, e: (0, i)),
        out_shape=jax.ShapeDtypeStruct((d, n), F32),
        compiler_params=_cparams(("parallel", "arbitrary")),
        name="peer_dense",
    )(hb, u, vt, a_t, scores, thr)


def _ln2_kernel(h_ref, pt_ref, g_ref, b_ref, o_ref, ob_ref):
    y = _layer_norm(ALPHA * h_ref[...] + pt_ref[...].T, g_ref[...], b_ref[...])
    o_ref[...] = y
    ob_ref[...] = y.astype(BF16)


def _peer_ln(h, peer_t, g, b, tm=256):
    n, d = h.shape
    return pl.pallas_call(
        _ln2_kernel,
        grid=(n // tm,),
        in_specs=[pl.BlockSpec((tm, d), lambda i: (i, 0)),
                  pl.BlockSpec((d, tm), lambda i: (0, i)),
                  pl.BlockSpec((1, d), lambda i: (0, 0)),
                  pl.BlockSpec((1, d), lambda i: (0, 0))],
        out_specs=[pl.BlockSpec((tm, d), lambda i: (i, 0)),
                   pl.BlockSpec((tm, d), lambda i: (i, 0))],
        out_shape=[jax.ShapeDtypeStruct((n, d), F32), jax.ShapeDtypeStruct((n, d), BF16)],
        compiler_params=_cparams(("parallel",)),
        name="peer_ln",
    )(h, peer_t, g.reshape(1, d), b.reshape(1, d))


def _prep_w_in(w):
    base = SSD_INNER + CONV_CH
    dt0 = w[:, base:base + SSD_HEADS]
    dt1 = w[:, base + SSD_HEADS:base + 2 * SSD_HEADS]
    zpad = jnp.zeros((w.shape[0], LANE - SSD_HEADS), w.dtype)
    w_a = jnp.concatenate([w[:, :base], dt0, zpad, dt1, zpad], axis=1).astype(BF16)
    w_b = w[:, base + 2 * SSD_HEADS:].astype(BF16)
    return w_a, w_b


def _pad_heads(p):
    return jnp.pad(p.astype(F32), ((0, 0), (0, LANE - SSD_HEADS)))[:, None, :]


def _token_mixer(h, hb, segs, w_in, conv_w, conv_b, dt_bias, a_log, d_skip, ssd_norm_g, diff_lambda,
                 diff_norm_g, lambda_init):
    w_a, w_b = _prep_w_in(w_in)
    proj_a = _matmul(hb, w_a, F32, 1024, 256, "proj_ssd")
    qkv = _matmul(hb, w_b, BF16, 1024, 512, "proj_attn")
    xbc_act = _ssd_conv(proj_a, conv_w, conv_b, segs)
    dtb, alog = _pad_heads(dt_bias), _pad_heads(a_log)
    y_fwd = _ssd_scan(0, xbc_act, proj_a, dtb, alog, segs)
    dsk = jnp.repeat(d_skip.astype(F32), SSD_HEADDIM).reshape(1, SSD_INNER)
    y_ssd = _ssd_scan(1, xbc_act, proj_a, dtb, alog, segs,
                      extra=(y_fwd, dsk, ssd_norm_g.reshape(1, SSD_INNER)))
    y_diff = _diff_attention(qkv, diff_lambda, diff_norm_g, lambda_init, segs)
    y_dil = _dilated_attention(qkv, segs)
    return y_ssd, y_diff, y_dil


def _peer(h, hb, wq, keys, u, v, g, b):
    scores = _peer_scores(hb, wq.astype(BF16), keys.astype(BF16))
    a_t, thr = _peer_thresholds(scores)
    peer_t = _peer_dense(hb, u.astype(BF16), v.astype(BF16).T, a_t, scores, thr)
    return _peer_ln(h, peer_t, g, b)


def _forward(x, segs, ln_in_g, ln_in_b, w_in, conv_w, conv_b, dt_bias, a_log, d_skip, ssd_norm_g,
             diff_lambda, diff_norm_g, w_out, ln1_g, ln1_b, peer_wq, peer_keys, peer_u, peer_v,
             ln2_g, ln2_b):
    h, hb = _ln_in(x, ln_in_g, ln_in_b)
    for l in range(DEPTH):
        lambda_init = 0.8 - 0.6 * math.exp(-0.3 * l)
        y_ssd, y_diff, y_dil = _token_mixer(
            h, hb, segs, w_in[l], conv_w[l], conv_b[l], dt_bias[l], a_log[l], d_skip[l],
            ssd_norm_g[l], diff_lambda[l], diff_norm_g[l], lambda_init)
        wo = w_out[l].astype(BF16)
        h, hb = _outproj_ln(y_ssd, y_diff, y_dil, wo[:SSD_INNER], wo[SSD_INNER:SSD_INNER + DIFF_WIDTH],
                            wo[SSD_INNER + DIFF_WIDTH:], h, ln1_g[l], ln1_b[l])
        h, hb = _peer(h, hb, peer_wq[l], peer_keys[l], peer_u[l], peer_v[l], ln2_g[l], ln2_b[l])
    return h


def kernel(x_prompt, x_sample, ln_in_g, ln_in_b, w_in, conv_w, conv_b, dt_bias, a_log, d_skip,
           ssd_norm_g, diff_lambda, diff_norm_g, w_out, ln1_g, ln1_b, peer_wq, peer_keys, peer_u,
           peer_v, ln2_g, ln2_b):
    bp, tp, d = x_prompt.shape
    bs, ts, _ = x_sample.shape
    np_, ns = bp * tp, bs * ts
    x = jnp.concatenate([x_prompt.reshape(np_, d), x_sample.reshape(ns, d)], axis=0)
    segs = tuple((b * tp, tp) for b in range(bp)) + tuple((np_ + b * ts, ts) for b in range(bs))
    y = _forward(x, segs, ln_in_g, ln_in_b, w_in, conv_w, conv_b, dt_bias, a_log, d_skip, ssd_norm_g,
                 diff_lambda, diff_norm_g, w_out, ln1_g, ln1_b, peer_wq, peer_keys, peer_u, peer_v,
                 ln2_g, ln2_b)
    return y[:np_].reshape(bp, tp, d), y[np_:].reshape(bs, ts, d)
```
